```python
import math
import jax
import jax.numpy as jnp
from jax import lax
import numpy as np

D_MODEL = 2048
BATCH = 2
SEQ = 8192
DEPTH = 1
DEC_BATCH = 128
DEC_SEQ = 1
PAST_LEN = 16384
PAGE_SIZE = 128

N_HEADS = 32
HEAD_DIM = 64
N_KV_HEADS = 4
GQA_GROUP = N_HEADS // N_KV_HEADS
D_ATTN = N_HEADS * HEAD_DIM
D_KV = N_KV_HEADS * HEAD_DIM
WINDOW = 128
BLOCK = 128
N_BUCKETS = 32
MAX_DISTANCE = WINDOW
SG_GROUPS = 16
SG_CHUNK = 128
D_SG = D_MODEL
SG_CH = D_SG // SG_GROUPS
D_FF = ((8 * D_MODEL // 3 + 255) // 256) * 256
CONV_W = 3
N_MOD = 6
EPS = 1e-6
NEG = -1e30
D_IN = D_ATTN + 2 * D_KV + 2 * D_SG + 2 * D_MODEL
SPLITS = (D_ATTN, D_ATTN + D_KV, D_ATTN + 2 * D_KV, D_ATTN + 2 * D_KV + D_SG,
          D_ATTN + 2 * D_KV + 2 * D_SG, D_ATTN + 2 * D_KV + 2 * D_SG + D_MODEL)

kernel_name = 'hybrid_swa_sink_gmlp_convffn_step'


def _rmsnorm(x, g):
    xf = x.astype(jnp.float32)
    y = xf * lax.rsqrt(jnp.mean(xf * xf, axis=-1, keepdims=True) + EPS)
    return (y * g.astype(jnp.float32)).astype(x.dtype)


def _layernorm(x, g, b):
    xf = x.astype(jnp.float32)
    xc = xf - jnp.mean(xf, axis=-1, keepdims=True)
    y = xc * lax.rsqrt(jnp.mean(xc * xc, axis=-1, keepdims=True) + EPS)
    return (y * g.astype(jnp.float32) + b.astype(jnp.float32)).astype(x.dtype)


def _t5_bucket(dist):
    max_exact = N_BUCKETS // 2
    d = jnp.maximum(dist, 0)
    ratio = jnp.log(jnp.maximum(d, 1).astype(jnp.float32) / max_exact) / math.log(MAX_DISTANCE / max_exact)
    large = jnp.minimum(max_exact + (ratio * (N_BUCKETS - max_exact)).astype(jnp.int32), N_BUCKETS - 1)
    return jnp.where(d < max_exact, d, large)


def _rel_bias(rel_bias, dist):
    b = rel_bias.astype(jnp.float32)[_t5_bucket(dist)]
    return jnp.transpose(b, (2, 0, 1)).reshape(N_KV_HEADS, GQA_GROUP, dist.shape[0], dist.shape[1])


def _sink_softmax(s, sinks):
    sink = sinks.astype(jnp.float32).reshape(N_KV_HEADS, GQA_GROUP, 1, 1)
    m = jnp.maximum(jnp.max(s, axis=-1, keepdims=True), sink)
    p = jnp.exp(s - m)
    return p / (jnp.sum(p, axis=-1, keepdims=True) + jnp.exp(sink - m))


def _swa_prompt(q, k, v, sinks, rel_bias):
    B, S = q.shape[0], q.shape[1]
    nb = S // BLOCK
    qb = q.reshape(B, nb, BLOCK, N_KV_HEADS, GQA_GROUP, HEAD_DIM)
    kb = k.reshape(B, nb, BLOCK, N_KV_HEADS, HEAD_DIM)
    vb = v.reshape(B, nb, BLOCK, N_KV_HEADS, HEAD_DIM)
    pad = ((0, 0), (1, 0), (0, 0), (0, 0), (0, 0))
    kk = jnp.concatenate([jnp.pad(kb, pad)[:, :-1], kb], axis=2)
    vv = jnp.concatenate([jnp.pad(vb, pad)[:, :-1], vb], axis=2)
    qi = jnp.arange(BLOCK)[:, None] + BLOCK
    kj = jnp.arange(2 * BLOCK)[None, :]
    dist = qi - kj
    in_window = (dist >= 0) & (dist <= WINDOW)
    valid = in_window[None] & ((jnp.arange(nb)[:, None, None] > 0) | (kj[None] >= BLOCK))
    s = jnp.einsum('bnqkgd,bnskd->bnkgqs', qb, kk, preferred_element_type=jnp.float32) * (HEAD_DIM ** -0.5)
    s = s + _rel_bias(rel_bias, dist)
    s = jnp.where(valid[None, :, None, None], s, NEG)
    p = _sink_softmax(s, sinks).astype(v.dtype)
    o = jnp.einsum('bnkgqs,bnskd->bnqkgd', p, vv).reshape(B, S, D_ATTN)
    return o, k[:, S - WINDOW:], v[:, S - WINDOW:]


def _swa_sample(q, k, v, k_buf, v_buf, sinks, rel_bias):
    B, T = q.shape[0], q.shape[1]
    kk = jnp.concatenate([k_buf.astype(k.dtype), k], axis=1)
    vv = jnp.concatenate([v_buf.astype(v.dtype), v], axis=1)
    dist = (jnp.arange(T)[:, None] + WINDOW) - jnp.arange(WINDOW + T)[None, :]
    valid = (dist >= 0) & (dist <= WINDOW)
    qg = q.reshape(B, T, N_KV_HEADS, GQA_GROUP, HEAD_DIM)
    s = jnp.einsum('btkgd,bskd->bkgts', qg, kk, preferred_element_type=jnp.float32) * (HEAD_DIM ** -0.5)
    s = s + _rel_bias(rel_bias, dist)
    s = jnp.where(valid, s, NEG)
    p = _sink_softmax(s, sinks).astype(v.dtype)
    o = jnp.einsum('bkgts,bskd->btkgd', p, vv).reshape(B, T, D_ATTN)
    return o, kk[:, T:], vv[:, T:]


def _spatial_gate(u, vn, w_s, b_s):
    T = u.shape[2]
    w = jnp.where(jnp.tril(jnp.ones((T, T), dtype=bool))[None], w_s[:, :T, :T], 0)
    s = jnp.einsum('gij,bnjgc->bnigc', w, vn) + b_s[:, :T].T[None, None, :, :, None]
    return u * s


def _conv_ffn(h, w_up, w_gate, conv_w, conv_b, w_down, conv_buf):
    T = h.shape[1]
    a = h @ w_gate
    u = h @ w_up
    if conv_buf is None:
        ap = jnp.pad(a, ((0, 0), (CONV_W - 1, 0), (0, 0)))
    else:
        ap = jnp.concatenate([conv_buf.astype(a.dtype), a], axis=1)
    conv = conv_b
    for tap in range(CONV_W):
        conv = conv + conv_w[tap] * ap[:, tap:tap + T]
    hid = jax.nn.gelu(conv) * u
    return hid @ w_down, ap[:, T:]


def _layer(x, c, w_ada, b_ada, g_pre_mix, g_post_mix, g_pre_ffn, g_post_ffn, w_in, w_o, sinks, rel_bias,
           sg_ln_g, sg_ln_b, w_s, b_s, w_up, w_gate, conv_w, conv_b, w_down, k_buf, v_buf, conv_buf):
    B, T = x.shape[0], x.shape[1]
    mod = jax.nn.silu(c) @ w_ada + b_ada
    sh_m, sc_m, gt_m, sh_f, sc_f, gt_f = jnp.split(mod[:, None, :], N_MOD, axis=-1)
    h = _rmsnorm(x, g_pre_mix) * (1 + sc_m) + sh_m
    z = h @ w_in
    q, k, v, u, vg, ga, gb = jnp.split(z, SPLITS, axis=-1)
    q = q.reshape(B, T, N_HEADS, HEAD_DIM)
    k = k.reshape(B, T, N_KV_HEADS, HEAD_DIM)
    v = v.reshape(B, T, N_KV_HEADS, HEAD_DIM)
    if k_buf is None:
        y_a, k_new, v_new = _swa_prompt(q, k, v, sinks, rel_bias)
        chunk = SG_CHUNK
    else:
        y_a, k_new, v_new = _swa_sample(q, k, v, k_buf, v_buf, sinks, rel_bias)
        chunk = T
    u = jax.nn.gelu(u)
    vn = _layernorm(jax.nn.gelu(vg), sg_ln_g, sg_ln_b)
    shp = (B, T // chunk, chunk, SG_GROUPS, SG_CH)
    y_b = _spatial_gate(u.reshape(shp), vn.reshape(shp), w_s, b_s).reshape(B, T, D_SG)
    merged = jax.nn.sigmoid(ga) * y_a + jax.nn.sigmoid(gb) * y_b
    x = x + gt_m * _rmsnorm(merged @ w_o, g_post_mix)
    h2 = _rmsnorm(x, g_pre_ffn) * (1 + sc_f) + sh_f
    y_f, conv_new = _conv_ffn(h2, w_up, w_gate, conv_w, conv_b, w_down, conv_buf)
    x = x + gt_f * _rmsnorm(y_f, g_post_ffn)
    return x, k_new, v_new, conv_new, vn


def setup_inputs(seed: int = 0) -> dict:
    key = jax.random.key(seed)
    ks = jax.random.split(key, 32)

    def nrm(k, shape, scale=1.0):
        return scale * jax.random.normal(k, shape, jnp.float32)

    L = DEPTH
    return {
        'x_prompt': nrm(ks[0], (BATCH, SEQ, D_MODEL)),
        'x_sample': nrm(ks[1], (DEC_BATCH, DEC_SEQ, D_MODEL)),
        'c_prompt': nrm(ks[2], (BATCH, D_MODEL)),
        'c_sample': nrm(ks[3], (DEC_BATCH, D_MODEL)),
        'cache_k_win': nrm(ks[4], (L, DEC_BATCH, WINDOW, N_KV_HEADS, HEAD_DIM)),
        'cache_v_win': nrm(ks[5], (L, DEC_BATCH, WINDOW, N_KV_HEADS, HEAD_DIM)),
        'state_ffn_conv': nrm(ks[6], (L, DEC_BATCH, CONV_W - 1, D_FF)),
        'w_ada': nrm(ks[7], (L, D_MODEL, N_MOD * D_MODEL), 0.5 * D_MODEL ** -0.5),
        'b_ada': nrm(ks[8], (L, N_MOD * D_MODEL), 0.01),
        'g_pre_mix': 1.0 + nrm(ks[9], (L, D_MODEL), 0.05),
        'g_post_mix': 1.0 + nrm(ks[10], (L, D_MODEL), 0.05),
        'g_pre_ffn': 1.0 + nrm(ks[11], (L, D_MODEL), 0.05),
        'g_post_ffn': 1.0 + nrm(ks[12], (L, D_MODEL), 0.05),
        'w_in': nrm(ks[13], (L, D_MODEL, D_IN), D_MODEL ** -0.5),
        'w_o': nrm(ks[14], (L, D_MODEL, D_MODEL), D_MODEL ** -0.5),
        'sinks': nrm(ks[15], (L, N_HEADS)),
        'rel_bias': nrm(ks[16], (N_BUCKETS, N_HEADS), 0.5),
        'sg_ln_g': 1.0 + nrm(ks[17], (L, D_SG), 0.05),
        'sg_ln_b': nrm(ks[18], (L, D_SG), 0.02),
        'w_s': nrm(ks[19], (L, SG_GROUPS, SG_CHUNK, SG_CHUNK), SG_CHUNK ** -0.5),
        'b_s': 1.0 + nrm(ks[20], (L, SG_GROUPS, SG_CHUNK), 0.1),
        'w_up': nrm(ks[21], (L, D_MODEL, D_FF), D_MODEL ** -0.5),
        'w_gate': nrm(ks[22], (L, D_MODEL, D_FF), D_MODEL ** -0.5),
        'conv_w': nrm(ks[23], (L, CONV_W, D_FF), CONV_W ** -0.5),
        'conv_b': nrm(ks[24], (L, D_FF), 0.01),
        'w_down': nrm(ks[25], (L, D_FF, D_MODEL), D_FF ** -0.5),
    }


def reference(x_prompt, x_sample, c_prompt, c_sample, cache_k_win, cache_v_win, state_ffn_conv,
              w_ada, b_ada, g_pre_mix, g_post_mix, g_pre_ffn, g_post_ffn, w_in, w_o, sinks, rel_bias,
              sg_ln_g, sg_ln_b, w_s, b_s, w_up, w_gate, conv_w, conv_b, w_down):
    y_p, y_s = x_prompt, x_sample
    kp_l, vp_l, cp_l, ks_l, vs_l, cs_l, sg_l = [], [], [], [], [], [], []
    for l in range(DEPTH):
        lw = (w_ada[l], b_ada[l], g_pre_mix[l], g_post_mix[l], g_pre_ffn[l], g_post_ffn[l], w_in[l], w_o[l],
              sinks[l], rel_bias, sg_ln_g[l], sg_ln_b[l], w_s[l], b_s[l], w_up[l], w_gate[l], conv_w[l],
              conv_b[l], w_down[l])
        y_p, kp, vp, cp, _ = _layer(y_p, c_prompt, *lw, None, None, None)
        y_s, ksn, vsn, csn, sgv = _layer(y_s, c_sample, *lw, cache_k_win[l], cache_v_win[l], state_ffn_conv[l])
        kp_l.append(kp)
        vp_l.append(vp)
        cp_l.append(cp)
        ks_l.append(ksn)
        vs_l.append(vsn)
        cs_l.append(csn)
        sg_l.append(sgv)
    return (y_p, y_s, jnp.stack(kp_l), jnp.stack(vp_l), jnp.stack(cp_l),
            jnp.stack(ks_l), jnp.stack(vs_l), jnp.stack(cs_l), jnp.stack(sg_l))
```

```python
import functools
import math

import jax
import jax.numpy as jnp
from jax import lax
from jax.experimental import pallas as pl
from jax.experimental.pallas import tpu as pltpu

N_HEADS = 32
HEAD_DIM = 64
N_KV_HEADS = 4
GQA_GROUP = N_HEADS // N_KV_HEADS
D_ATTN = N_HEADS * HEAD_DIM
D_KV = N_KV_HEADS * HEAD_DIM
WINDOW = 128
BLOCK = 128
N_BUCKETS = 32
MAX_DISTANCE = WINDOW
SG_GROUPS = 16
SG_CHUNK = 128
CONV_W = 3
N_MOD = 6
EPS = 1e-6
NEG = -1e30
Q_SCALE = HEAD_DIM ** -0.5

F32 = jnp.float32
BF16 = jnp.bfloat16

V7X_VMEM_BYTES = 64 * 1024 * 1024
VMEM_LIMIT_BYTES = V7X_VMEM_BYTES - 8 * 1024 * 1024
SUBLANES = 8
LANES = 128


def _params(n_axes):
    return pltpu.CompilerParams(dimension_semantics=("arbitrary",) * n_axes,
                                vmem_limit_bytes=VMEM_LIMIT_BYTES)


def _tile(n, pref):
    t = min(n, pref)
    while n % t:
        t //= 2
    return t


def _rms(x):
    return x * lax.rsqrt(jnp.mean(x * x, axis=-1, keepdims=True) + EPS)


def _gelu(x):
    return jax.nn.gelu(x, approximate=True)


def _mod_kernel(c_ref, w_ref, b_ref, o_ref):
    c = c_ref[...]
    a = (c * jax.nn.sigmoid(c)).astype(BF16)
    o_ref[...] = jnp.dot(a, w_ref[...].astype(BF16), preferred_element_type=F32) + b_ref[...]


def _modulation(c_all, w_ada, b_ada):
    rows, d = c_all.shape
    n_out = w_ada.shape[1]
    tn = _tile(n_out, 1024)
    return pl.pallas_call(
        _mod_kernel,
        grid=(n_out // tn,),
        in_specs=[pl.BlockSpec((rows, d), lambda n: (0, 0)),
                  pl.BlockSpec((d, tn), lambda n: (0, n)),
                  pl.BlockSpec((1, tn), lambda n: (0, n))],
        out_specs=pl.BlockSpec((rows, tn), lambda n: (0, n)),
        out_shape=jax.ShapeDtypeStruct((rows, n_out), F32),
        compiler_params=_params(1),
        name="mod",
    )(c_all, w_ada, b_ada.reshape(1, n_out))


def _t5_bucket(dist):
    max_exact = N_BUCKETS // 2
    d = jnp.maximum(dist, 0)
    ratio = jnp.log(jnp.maximum(d, 1).astype(F32) / max_exact) / math.log(MAX_DISTANCE / max_exact)
    large = jnp.minimum(max_exact + (ratio * (N_BUCKETS - max_exact)).astype(jnp.int32), N_BUCKETS - 1)
    return jnp.where(d < max_exact, d, large)


def _bias_kernel(rel_ref, bucket_ref, o_ref):
    h = pl.program_id(0)
    bkt = bucket_ref[...]
    acc = jnp.where(bkt < 0, NEG, 0.0).astype(F32)
    for b in range(N_BUCKETS):
        acc = jnp.where(bkt == b, rel_ref[b * N_HEADS + h], acc)
    o_ref[...] = acc


def _bias_table(rel_bias):
    qi = jnp.arange(BLOCK)[:, None] + BLOCK
    kj = jnp.arange(2 * BLOCK)[None, :]
    dist = qi - kj
    bucket = jnp.where((dist >= 0) & (dist <= WINDOW), _t5_bucket(dist), -1).astype(jnp.int32)
    return pl.pallas_call(
        _bias_kernel,
        grid=(N_HEADS,),
        in_specs=[pl.BlockSpec(memory_space=pltpu.SMEM),
                  pl.BlockSpec((BLOCK, 2 * BLOCK), lambda h: (0, 0))],
        out_specs=pl.BlockSpec((None, BLOCK, 2 * BLOCK), lambda h: (h, 0, 0)),
        out_shape=jax.ShapeDtypeStruct((N_HEADS, BLOCK, 2 * BLOCK), F32),
        compiler_params=_params(1),
        name="bias",
    )(rel_bias.astype(F32).reshape(-1), bucket)


def _inproj_kernel(x_ref, sc_ref, sh_ref, g_ref, w_ref,
                   q_ref, kv_ref, u_ref, vg_ref, ga_ref, gb_ref, kvw_ref, h_scr,
                   *, seg, tiles_per_seq):
    i = pl.program_id(0)
    n = pl.program_id(1)

    @pl.when(n == 0)
    def _():
        y = _rms(x_ref[...]) * g_ref[...]
        h_scr[...] = (y * (1.0 + sc_ref[...]) + sh_ref[...]).astype(BF16)

    acc = jnp.dot(h_scr[...], w_ref[...], preferred_element_type=F32)
    n_q, n_kv, n_u, n_vg, n_ga = seg

    @pl.when(n < n_q)
    def _():
        q_ref[...] = (acc * Q_SCALE).astype(BF16)

    @pl.when(n == n_q)
    def _():
        kv_ref[...] = acc.astype(BF16)
        rows = kvw_ref.shape[0]

        @pl.when(i % tiles_per_seq == tiles_per_seq - 1)
        def _():
            kvw_ref[...] = acc[acc.shape[0] - rows:, :]

    @pl.when((n >= n_kv) & (n < n_u))
    def _():
        u_ref[...] = _gelu(acc).astype(BF16)

    @pl.when((n >= n_u) & (n < n_vg))
    def _():
        vg_ref[...] = _gelu(acc).astype(BF16)

    @pl.when((n >= n_vg) & (n < n_ga))
    def _():
        ga_ref[...] = jax.nn.sigmoid(acc).astype(BF16)

    @pl.when(n >= n_ga)
    def _():
        gb_ref[...] = jax.nn.sigmoid(acc).astype(BF16)


def _inproj(x, sc, sh, g, w_in, *, seq, per_row_mod):
    m, d = x.shape
    d_in = w_in.shape[1]
    d_sg = (d_in - D_ATTN - 2 * D_KV - 2 * d) // 2
    tn = 2 * D_KV
    assert D_ATTN % tn == 0 and d_sg % tn == 0 and d % tn == 0
    tm = _tile(seq, 1024)
    tiles_per_seq = seq // tm
    win = min(seq, WINDOW)
    n_q = D_ATTN // tn
    n_kv = n_q + 1
    n_u = n_kv + d_sg // tn
    n_vg = n_u + d_sg // tn
    n_ga = n_vg + d // tn
    n_all = d_in // tn

    def seg_map(lo, hi):
        return lambda i, n: (i, jnp.clip(n - lo, 0, hi - lo - 1))

    if per_row_mod:
        mod_spec = pl.BlockSpec((tm, d), lambda i, n: (i, 0))
    else:
        mod_spec = pl.BlockSpec((None, 1, d), lambda i, n: (i // tiles_per_seq, 0, 0))
    out_bf = lambda width: jax.ShapeDtypeStruct((m, width), BF16)
    return pl.pallas_call(
        functools.partial(_inproj_kernel, seg=(n_q, n_kv, n_u, n_vg, n_ga), tiles_per_seq=tiles_per_seq),
        grid=(m // tm, n_all),
        in_specs=[pl.BlockSpec((tm, d), lambda i, n: (i, 0)),
                  mod_spec, mod_spec,
                  pl.BlockSpec((1, d), lambda i, n: (0, 0)),
                  pl.BlockSpec((d, tn), lambda i, n: (0, n))],
        out_specs=[pl.BlockSpec((tm, tn), seg_map(0, n_q)),
                   pl.BlockSpec((tm, tn), lambda i, n: (i, 0)),
                   pl.BlockSpec((tm, tn), seg_map(n_kv, n_u)),
                   pl.BlockSpec((tm, tn), seg_map(n_u, n_vg)),
                   pl.BlockSpec((tm, tn), seg_map(n_vg, n_ga)),
                   pl.BlockSpec((tm, tn), seg_map(n_ga, n_all)),
                   pl.BlockSpec((None, win, tn), lambda i, n: (i // tiles_per_seq, 0, 0))],
        out_shape=[out_bf(D_ATTN), out_bf(tn), out_bf(d_sg), out_bf(d_sg), out_bf(d), out_bf(d),
                   jax.ShapeDtypeStruct((m // seq, win, tn), F32)],
        scratch_shapes=[pltpu.VMEM((tm, d), BF16)],
        compiler_params=_params(2),
        name="inproj",
    )(x, sc, sh, g.reshape(1, d), w_in)


def _layernorm_rows(v, g, b):
    vc = v - jnp.mean(v, axis=-1, keepdims=True)
    return vc * lax.rsqrt(jnp.mean(vc * vc, axis=-1, keepdims=True) + EPS) * g + b


def _mixer_kernel(sinks_ref, q_ref, kv_ref, kvp_ref, u_ref, vg_ref, ga_ref, gb_ref,
                  bias_ref, ws_ref, bs_ref, lng_ref, lnb_ref, o_ref,
                  kk_scr, wt_scr, ya_scr):
    j = pl.program_id(1)
    tq = q_ref.shape[0]
    kk_scr[0:BLOCK, :] = kvp_ref[...]
    kk_scr[BLOCK:, :] = kv_ref[...]
    row = lax.broadcasted_iota(jnp.int32, (SG_CHUNK, SG_CHUNK), 0)
    col = lax.broadcasted_iota(jnp.int32, (SG_CHUNK, SG_CHUNK), 1)
    for g in range(SG_GROUPS):
        wt_scr[g] = jnp.where(row >= col, ws_ref[g], 0.0).astype(BF16)

    def block(bi, carry):
        r0 = pl.multiple_of(bi * BLOCK, BLOCK)
        kblk = kk_scr[pl.ds(r0, 2 * BLOCK), :]
        qblk = q_ref[pl.ds(r0, BLOCK), :]
        prev_neg = jnp.where((j == 0) & (bi == 0), NEG, 0.0).astype(F32)
        for kvh in range(N_KV_HEADS):
            kg = kblk[:, kvh * HEAD_DIM:(kvh + 1) * HEAD_DIM]
            vg = kblk[:, D_KV + kvh * HEAD_DIM:D_KV + (kvh + 1) * HEAD_DIM]
            for hh in range(GQA_GROUP):
                h = kvh * GQA_GROUP + hh
                qh = qblk[:, h * HEAD_DIM:(h + 1) * HEAD_DIM]
                s = lax.dot_general(qh, kg, (((1,), (1,)), ((), ())), preferred_element_type=F32)
                s = s + bias_ref[h]
                s = jnp.concatenate([s[:, :BLOCK] + prev_neg, s[:, BLOCK:]], axis=1)
                sink = sinks_ref[h]
                mx = jnp.maximum(jnp.max(s, axis=-1, keepdims=True), sink)
                p = jnp.exp(s - mx)
                den = jnp.sum(p, axis=-1, keepdims=True) + jnp.exp(sink - mx)
                o = jnp.dot(p.astype(BF16), vg, preferred_element_type=F32)
                ya_scr[:, h * HEAD_DIM:(h + 1) * HEAD_DIM] = o / den
        vn = _layernorm_rows(vg_ref[pl.ds(r0, BLOCK), :].astype(F32), lng_ref[...], lnb_ref[...]).astype(BF16)
        for g in range(SG_GROUPS):
            cs = slice(g * SG_CHUNK, (g + 1) * SG_CHUNK)
            gate = jnp.dot(wt_scr[g], vn[:, cs], preferred_element_type=F32) + bs_ref[g]
            yb = u_ref[pl.ds(r0, BLOCK), cs].astype(F32) * gate
            merged = (ga_ref[pl.ds(r0, BLOCK), cs].astype(F32) * ya_scr[:, cs]
                      + gb_ref[pl.ds(r0, BLOCK), cs].astype(F32) * yb)
            o_ref[pl.ds(r0, BLOCK), cs] = merged.astype(BF16)
        return carry

    lax.fori_loop(0, tq // BLOCK, block, 0)


def _mixer_prompt(q, kv, u, vg, ga, gb, bias_tab, sinks, w_s, b_s, ln_g, ln_b, *, batch, seq):
    m, d = u.shape
    tq = _tile(seq, 512)
    steps = seq // tq
    blocks_per_tile = tq // BLOCK
    assert d == SG_GROUPS * SG_CHUNK and d == D_ATTN
    row_spec = lambda width: pl.BlockSpec((tq, width), lambda b, j: (b * steps + j, 0))
    const3 = lambda shape: pl.BlockSpec(shape, lambda b, j: (0, 0, 0))
    bs_b = jnp.broadcast_to(b_s.astype(F32)[:, :, None], (SG_GROUPS, SG_CHUNK, SG_CHUNK))
    return pl.pallas_call(
        _mixer_kernel,
        grid=(batch, steps),
        in_specs=[pl.BlockSpec(memory_space=pltpu.SMEM),
                  row_spec(D_ATTN), row_spec(2 * D_KV),
                  pl.BlockSpec((BLOCK, 2 * D_KV),
                               lambda b, j: (jnp.maximum((b * steps + j) * blocks_per_tile - 1, 0), 0)),
                  row_spec(d), row_spec(d), row_spec(d), row_spec(d),
                  const3((N_HEADS, BLOCK, 2 * BLOCK)),
                  const3((SG_GROUPS, SG_CHUNK, SG_CHUNK)),
                  const3((SG_GROUPS, SG_CHUNK, SG_CHUNK)),
                  pl.BlockSpec((1, d), lambda b, j: (0, 0)),
                  pl.BlockSpec((1, d), lambda b, j: (0, 0))],
        out_specs=row_spec(d),
        out_shape=jax.ShapeDtypeStruct((m, d), BF16),
        scratch_shapes=[pltpu.VMEM((tq + BLOCK, 2 * D_KV), BF16),
                        pltpu.VMEM((SG_GROUPS, SG_CHUNK, SG_CHUNK), BF16),
                        pltpu.VMEM((BLOCK, D_ATTN), F32)],
        compiler_params=_params(2),
        name="mixer",
    )(sinks.astype(F32), q, kv, kv, u, vg, ga, gb, bias_tab, w_s.astype(F32), bs_b,
      ln_g.reshape(1, d), ln_b.reshape(1, d))


def _mixer_sample_kernel(q_ref, kn_ref, vn_ref, ck_ref, cv_ref, u_ref, vg_ref, ga_ref, gb_ref,
                         biasc_ref, biasn_ref, sinks_ref, wrow_ref, brow_ref, lng_ref, lnb_ref,
                         o_ref, sg_ref, ya_scr):
    for kvh in range(N_KV_HEADS):
        hs = slice(kvh * GQA_GROUP, (kvh + 1) * GQA_GROUP)
        ds_ = slice(kvh * HEAD_DIM, (kvh + 1) * HEAD_DIM)
        qg = q_ref[:, hs, :]
        kc = ck_ref[:, :, ds_].astype(BF16)
        s = jnp.einsum('bhd,bkd->bhk', qg.astype(BF16), kc, preferred_element_type=F32) + biasc_ref[hs, :][None]
        k_new = kn_ref[:, kvh:kvh + 1, :]
        s_new = jnp.sum(qg * k_new, axis=-1, keepdims=True) + biasn_ref[hs, :][None]
        sink = sinks_ref[hs, :][None]
        mx = jnp.maximum(jnp.maximum(jnp.max(s, axis=-1, keepdims=True), s_new), sink)
        p = jnp.exp(s - mx)
        p_new = jnp.exp(s_new - mx)
        den = jnp.sum(p, axis=-1, keepdims=True) + p_new + jnp.exp(sink - mx)
        vc = cv_ref[:, :, ds_].astype(BF16)
        o = jnp.einsum('bhk,bkd->bhd', p.astype(BF16), vc, preferred_element_type=F32)
        o = o + p_new * vn_ref[:, kvh:kvh + 1, :]
        ya_scr[:, hs, :] = o / den
    v = vg_ref[...]
    n_ch = v.shape[1] * v.shape[2]
    mean = jnp.sum(jnp.sum(v, axis=2, keepdims=True), axis=1, keepdims=True) / n_ch
    vc = v - mean
    var = jnp.sum(jnp.sum(vc * vc, axis=2, keepdims=True), axis=1, keepdims=True) / n_ch
    vn = vc * lax.rsqrt(var + EPS) * lng_ref[...][None] + lnb_ref[...][None]
    sg_ref[...] = vn
    yb = u_ref[...] * (wrow_ref[...][None] * vn + brow_ref[...][None])
    merged = ga_ref[...] * ya_scr[...] + gb_ref[...] * yb
    o_ref[...] = merged.astype(BF16)


def _mixer_sample(q, k_new, v_new, cache_k, cache_v, u, vg, ga, gb, bias_tab, sinks, w_s, b_s, ln_g, ln_b):
    nb, d = u.shape
    assert d == D_ATTN
    bt = _tile(nb, 16)
    heads = lambda a: a.astype(F32).reshape(nb, N_HEADS, HEAD_DIM)
    kvh = lambda a: a.reshape(nb, N_KV_HEADS, HEAD_DIM)
    per_head = lambda a: jnp.repeat(a.astype(F32), N_HEADS // SG_GROUPS).reshape(N_HEADS, 1)
    h_spec = pl.BlockSpec((bt, N_HEADS, HEAD_DIM), lambda b: (b, 0, 0))
    kv_spec = pl.BlockSpec((bt, N_KV_HEADS, HEAD_DIM), lambda b: (b, 0, 0))
    c_spec = pl.BlockSpec((bt, WINDOW, D_KV), lambda b: (b, 0, 0))
    const = lambda shape: pl.BlockSpec(shape, lambda b: (0, 0))
    bias_c = bias_tab[:, 0, :WINDOW]
    bias_n = bias_tab[:, 0, WINDOW:WINDOW + 1]
    merged, sg_v = pl.pallas_call(
        _mixer_sample_kernel,
        grid=(nb // bt,),
        in_specs=[h_spec, kv_spec, kv_spec, c_spec, c_spec, h_spec, h_spec, h_spec, h_spec,
                  const((N_HEADS, WINDOW)), const((N_HEADS, 1)), const((N_HEADS, 1)),
                  const((N_HEADS, 1)), const((N_HEADS, 1)),
                  const((N_HEADS, HEAD_DIM)), const((N_HEADS, HEAD_DIM))],
        out_specs=[h_spec, h_spec],
        out_shape=[jax.ShapeDtypeStruct((nb, N_HEADS, HEAD_DIM), BF16),
                   jax.ShapeDtypeStruct((nb, N_HEADS, HEAD_DIM), F32)],
        scratch_shapes=[pltpu.VMEM((bt, N_HEADS, HEAD_DIM), F32)],
        compiler_params=_params(1),
        name="mixer_sample",
    )(heads(q), kvh(k_new), kvh(v_new), cache_k, cache_v, heads(u), heads(vg), heads(ga), heads(gb),
      bias_c, bias_n, sinks.astype(F32).reshape(N_HEADS, 1),
      per_head(w_s[:, 0, 0]), per_head(b_s[:, 0]),
      ln_g.astype(F32).reshape(N_HEADS, HEAD_DIM), ln_b.astype(F32).reshape(N_HEADS, HEAD_DIM))
    return merged.reshape(nb, d), sg_v.reshape(nb, d)


def _outproj_kernel(m_ref, w_ref, x_ref, gt_ref, gpost_ref, gpre_ref, sc_ref, sh_ref, x1_ref, h2_ref):
    o = jnp.dot(m_ref[...], w_ref[...], preferred_element_type=F32)
    x1 = x_ref[...] + gt_ref[...] * (_rms(o) * gpost_ref[...])
    x1_ref[...] = x1
    y = _rms(x1) * gpre_ref[...]
    h2_ref[...] = (y * (1.0 + sc_ref[...]) + sh_ref[...]).astype(BF16)


def _outproj(merged, w_o, x, gt, g_post, g_pre, sc, sh, *, seq, per_row_mod):
    m, d = x.shape
    tm = _tile(seq, 512)
    tiles_per_seq = seq // tm
    if per_row_mod:
        mod_spec = pl.BlockSpec((tm, d), lambda i: (i, 0))
    else:
        mod_spec = pl.BlockSpec((None, 1, d), lambda i: (i // tiles_per_seq, 0, 0))
    row_spec = pl.BlockSpec((tm, d), lambda i: (i, 0))
    vec_spec = pl.BlockSpec((1, d), lambda i: (0, 0))
    return pl.pallas_call(
        _outproj_kernel,
        grid=(m // tm,),
        in_specs=[row_spec, pl.BlockSpec((d, d), lambda i: (0, 0)), row_spec,
                  mod_spec, vec_spec, vec_spec, mod_spec, mod_spec],
        out_specs=[row_spec, row_spec],
        out_shape=[jax.ShapeDtypeStruct((m, d), F32), jax.ShapeDtypeStruct((m, d), BF16)],
        compiler_params=_params(1),
        name="outproj",
    )(merged, w_o, x, gt, g_post.reshape(1, d), g_pre.reshape(1, d), sc, sh)


def _ffn_finish(f, acc_scr, part, x1_ref, gt_ref, gpost_ref, y_ref):
    @pl.when(f == 0)
    def _():
        acc_scr[...] = part

    @pl.when(f > 0)
    def _():
        acc_scr[...] += part

    @pl.when(f == pl.num_programs(1) - 1)
    def _():
        y_ref[...] = x1_ref[...] + gt_ref[...] * (_rms(acc_scr[...]) * gpost_ref[...])


def _ffn_prompt_kernel(h_ref, wg_ref, wu_ref, cw_ref, cb_ref, wd_ref, x1_ref, gt_ref, gpost_ref,
                       y_ref, cs_ref, acc_scr, carry_scr, *, tiles_per_seq):
    i = pl.program_id(0)
    f = pl.program_id(1)
    h = h_ref[...]
    a = jnp.dot(h, wg_ref[...], preferred_element_type=F32)
    u = jnp.dot(h, wu_ref[...], preferred_element_type=F32)
    tm = a.shape[0]
    @pl.when(i % tiles_per_seq == 0)
    def _():
        carry_scr[f] = jnp.zeros(carry_scr.shape[1:], F32)

    carry = carry_scr[f]
    prev1 = carry[SUBLANES - 1:SUBLANES, :]
    prev2 = carry[SUBLANES - 2:SUBLANES - 1, :]
    row = lax.broadcasted_iota(jnp.int32, a.shape, 0)
    a1 = jnp.where(row == 0, prev1, pltpu.roll(a, 1, axis=0))
    a2 = jnp.where(row == 0, prev2, jnp.where(row == 1, prev1, pltpu.roll(a, 2, axis=0)))
    conv = cb_ref[...] + cw_ref[0:1, :] * a2 + cw_ref[1:2, :] * a1 + cw_ref[2:3, :] * a
    carry_scr[f] = a[tm - SUBLANES:, :]
    cs_ref[f] = a[tm - (CONV_W - 1):, :]
    hid = (_gelu(conv) * u).astype(BF16)
    part = jnp.dot(hid, wd_ref[...], preferred_element_type=F32)
    _ffn_finish(f, acc_scr, part, x1_ref, gt_ref, gpost_ref, y_ref)


def _ffn_sample_kernel(h_ref, wg_ref, wu_ref, cw_ref, cb_ref, wd_ref, x1_ref, gt_ref, gpost_ref,
                       st0_ref, st1_ref, y_ref, a_ref, acc_scr):
    f = pl.program_id(1)
    h = h_ref[...]
    a = jnp.dot(h, wg_ref[...], preferred_element_type=F32)
    u = jnp.dot(h, wu_ref[...], preferred_element_type=F32)
    conv = cb_ref[...] + cw_ref[0:1, :] * st0_ref[...] + cw_ref[1:2, :] * st1_ref[...] + cw_ref[2:3, :] * a
    a_ref[...] = a
    hid = (_gelu(conv) * u).astype(BF16)
    part = jnp.dot(hid, wd_ref[...], preferred_element_type=F32)
    _ffn_finish(f, acc_scr, part, x1_ref, gt_ref, gpost_ref, y_ref)


def _ffn(h2, w_gate, w_up, conv_w, conv_b, w_down, x1, gt, g_post, *, seq, state=None):
    m, d = x1.shape
    d_ff = w_up.shape[1]
    tf = _tile(d_ff, 512)
    n_f = d_ff // tf
    tm = _tile(seq, 512)
    tiles_per_seq = seq // tm
    sample = state is not None
    if sample:
        mod_spec = pl.BlockSpec((tm, d), lambda i, f: (i, 0))
    else:
        mod_spec = pl.BlockSpec((None, 1, d), lambda i, f: (i // tiles_per_seq, 0, 0))
    row_spec = pl.BlockSpec((tm, d), lambda i, f: (i, 0))
    col_w = pl.BlockSpec((d, tf), lambda i, f: (0, f))
    in_specs = [row_spec, col_w, col_w,
                pl.BlockSpec((CONV_W, tf), lambda i, f: (0, f)),
                pl.BlockSpec((1, tf), lambda i, f: (0, f)),
                pl.BlockSpec((tf, d), lambda i, f: (f, 0)),
                row_spec, mod_spec, pl.BlockSpec((1, d), lambda i, f: (0, 0))]
    args = [h2, w_gate, w_up, conv_w.astype(F32), conv_b.astype(F32).reshape(1, d_ff), w_down,
            x1, gt, g_post.reshape(1, d)]
    scratch = [pltpu.VMEM((tm, d), F32)]
    if sample:
        st_spec = pl.BlockSpec((tm, tf), lambda i, f: (i, f))
        in_specs += [st_spec, st_spec]
        args += [state[:, 0, :], state[:, 1, :]]
        body = _ffn_sample_kernel
        out_specs = [row_spec, st_spec]
        out_shape = [jax.ShapeDtypeStruct((m, d), F32), jax.ShapeDtypeStruct((m, d_ff), F32)]
    else:
        body = functools.partial(_ffn_prompt_kernel, tiles_per_seq=tiles_per_seq)
        out_specs = [row_spec,
                     pl.BlockSpec((None, n_f, CONV_W - 1, tf), lambda i, f: (i // tiles_per_seq, 0, 0, 0))]
        out_shape = [jax.ShapeDtypeStruct((m, d), F32),
                     jax.ShapeDtypeStruct((m // seq, n_f, CONV_W - 1, tf), F32)]
        scratch.append(pltpu.VMEM((n_f, SUBLANES, tf), F32))
    y, aux = pl.pallas_call(
        body,
        grid=(m // tm, n_f),
        in_specs=in_specs,
        out_specs=out_specs,
        out_shape=out_shape,
        scratch_shapes=scratch,
        compiler_params=_params(2),
        name="ffn_sample" if sample else "ffn",
    )(*args)
    if not sample:
        aux = aux.transpose(0, 2, 1, 3).reshape(m // seq, CONV_W - 1, d_ff)
    return y, aux


def kernel(x_prompt, x_sample, c_prompt, c_sample, cache_k_win, cache_v_win, state_ffn_conv, w_ada, b_ada, g_pre_mix, g_post_mix, g_pre_ffn, g_post_ffn, w_in, w_o, sinks, rel_bias, sg_ln_g, sg_ln_b, w_s, b_s, w_up, w_gate, conv_w, conv_b, w_down):
    depth = w_in.shape[0]
    assert depth == 1, "single-layer step"
    batch, seq, d = x_prompt.shape
    nb, dec_seq, _ = x_sample.shape
    assert dec_seq == 1 and seq % BLOCK == 0
    l = 0

    n_rows = batch + nb
    pad = (-n_rows) % SUBLANES
    c_all = jnp.concatenate([c_prompt, c_sample, jnp.zeros((pad, d), F32)], axis=0)
    mod = _modulation(c_all, w_ada[l], b_ada[l])
    mods = jnp.split(mod, N_MOD, axis=-1)
    mp = [t[:batch].reshape(batch, 1, d) for t in mods]
    ms = [t[batch:n_rows] for t in mods]

    bias_tab = _bias_table(rel_bias)
    w_in_b = w_in[l].astype(BF16)
    w_o_b = w_o[l].astype(BF16)
    w_up_b = w_up[l].astype(BF16)
    w_gate_b = w_gate[l].astype(BF16)
    w_down_b = w_down[l].astype(BF16)

    xp = x_prompt.reshape(batch * seq, d)
    q, kv, u, vg, ga, gb, kv_win = _inproj(xp, mp[1], mp[0], g_pre_mix[l], w_in_b, seq=seq, per_row_mod=False)
    merged = _mixer_prompt(q, kv, u, vg, ga, gb, bias_tab, sinks[l], w_s[l], b_s[l], sg_ln_g[l], sg_ln_b[l],
                           batch=batch, seq=seq)
    x1, h2 = _outproj(merged, w_o_b, xp, mp[2], g_post_mix[l], g_pre_ffn[l], mp[4], mp[3],
                      seq=seq, per_row_mod=False)
    yp, conv_p = _ffn(h2, w_gate_b, w_up_b, conv_w[l], conv_b[l], w_down_b, x1, mp[5], g_post_ffn[l], seq=seq)
    y_prompt = yp.reshape(batch, seq, d)
    k_win_p = kv_win[:, :, :D_KV].reshape(1, batch, WINDOW, N_KV_HEADS, HEAD_DIM)
    v_win_p = kv_win[:, :, D_KV:].reshape(1, batch, WINDOW, N_KV_HEADS, HEAD_DIM)

    xs = x_sample.reshape(nb, d)
    q, kv, u, vg, ga, gb, kv_new = _inproj(xs, ms[1], ms[0], g_pre_mix[l], w_in_b, seq=nb, per_row_mod=True)
    kv_new = kv_new.reshape(nb, 2 * D_KV)
    k_new, v_new = kv_new[:, :D_KV], kv_new[:, D_KV:]
    ck = cache_k_win[l].reshape(nb, WINDOW, D_KV)
    cv = cache_v_win[l].reshape(nb, WINDOW, D_KV)
    merged, sg_v = _mixer_sample(q, k_new, v_new, ck, cv, u, vg, ga, gb, bias_tab, sinks[l], w_s[l], b_s[l],
                                 sg_ln_g[l], sg_ln_b[l])
    x1, h2 = _outproj(merged, w_o_b, xs, ms[2], g_post_mix[l], g_pre_ffn[l], ms[4], ms[3],
                      seq=nb, per_row_mod=True)
    ys, a_new = _ffn(h2, w_gate_b, w_up_b, conv_w[l], conv_b[l], w_down_b, x1, ms[5], g_post_ffn[l],
                     seq=nb, state=state_ffn_conv[l])
    y_sample = ys.reshape(nb, 1, d)
    k_win_s = jnp.concatenate([ck[:, 1:], k_new[:, None, :]], axis=1).reshape(1, nb, WINDOW, N_KV_HEADS, HEAD_DIM)
    v_win_s = jnp.concatenate([cv[:, 1:], v_new[:, None, :]], axis=1).reshape(1, nb, WINDOW, N_KV_HEADS, HEAD_DIM)
    conv_s = jnp.stack([state_ffn_conv[l][:, 1, :], a_new], axis=1)[None]
    sg_s = sg_v.reshape(1, nb, 1, d)

    return (y_prompt, y_sample, k_win_p, v_win_p, conv_p[None], k_win_s, v_win_s, conv_s, sg_s)
```

```python
import functools
import math

import jax
import jax.numpy as jnp
from jax import lax
from jax.experimental import pallas as pl
from jax.experimental.pallas import tpu as pltpu

N_HEADS = 32
HEAD_DIM = 64
N_KV_HEADS = 4
GQA_GROUP = N_HEADS // N_KV_HEADS
D_ATTN = N_HEADS * HEAD_DIM
D_KV = N_KV_HEADS * HEAD_DIM
WINDOW = 128
BLOCK = 128
N_BUCKETS = 32
MAX_DISTANCE = WINDOW
SG_GROUPS = 16
SG_CHUNK = 128
CONV_W = 3
N_MOD = 6
EPS = 1e-6
NEG = -1e30
Q_SCALE = HEAD_DIM ** -0.5

F32 = jnp.float32
BF16 = jnp.bfloat16

V7X_VMEM_BYTES = 64 * 1024 * 1024
VMEM_LIMIT_BYTES = V7X_VMEM_BYTES - 8 * 1024 * 1024
SUBLANES = 8
LANES = 128
MXU_COLS = 256


def _params(n_axes):
    return pltpu.CompilerParams(dimension_semantics=("arbitrary",) * n_axes,
                                vmem_limit_bytes=VMEM_LIMIT_BYTES)


def _tile(n, pref):
    t = min(n, pref)
    while n % t:
        t //= 2
    return t


def _rms(x):
    return x * lax.rsqrt(jnp.mean(x * x, axis=-1, keepdims=True) + EPS)


def _gelu(x):
    return jax.nn.gelu(x, approximate=True)


def _mod_kernel(c_ref, w_ref, b_ref, o_ref):
    c = c_ref[...]
    a = (c * jax.nn.sigmoid(c)).astype(BF16)
    o_ref[...] = jnp.dot(a, w_ref[...].astype(BF16), preferred_element_type=F32) + b_ref[...]


def _modulation(c_all, w_ada, b_ada):
    rows, d = c_all.shape
    n_out = w_ada.shape[1]
    tn = _tile(n_out, 1024)
    return pl.pallas_call(
        _mod_kernel,
        grid=(n_out // tn,),
        in_specs=[pl.BlockSpec((rows, d), lambda n: (0, 0)),
                  pl.BlockSpec((d, tn), lambda n: (0, n)),
                  pl.BlockSpec((1, tn), lambda n: (0, n))],
        out_specs=pl.BlockSpec((rows, tn), lambda n: (0, n)),
        out_shape=jax.ShapeDtypeStruct((rows, n_out), F32),
        compiler_params=_params(1),
        name="mod",
    )(c_all, w_ada, b_ada.reshape(1, n_out))


def _t5_bucket(dist):
    max_exact = N_BUCKETS // 2
    d = jnp.maximum(dist, 0)
    ratio = jnp.log(jnp.maximum(d, 1).astype(F32) / max_exact) / math.log(MAX_DISTANCE / max_exact)
    large = jnp.minimum(max_exact + (ratio * (N_BUCKETS - max_exact)).astype(jnp.int32), N_BUCKETS - 1)
    return jnp.where(d < max_exact, d, large)


PAIRS = GQA_GROUP // 2


def _bias_kernel(rel_ref, bucket_ref, o_ref):
    h = pl.program_id(1)
    bkt = bucket_ref[...]
    acc = jnp.where(bkt < 0, NEG, 0.0).astype(F32)
    for b in range(N_BUCKETS):
        acc = jnp.where(bkt == b, rel_ref[b * N_HEADS + h], acc)
    o_ref[...] = acc


def _bias_table(rel_bias):
    qi = jnp.arange(BLOCK)[:, None] + BLOCK
    kj = jnp.arange(2 * BLOCK)[None, :]
    dist = qi - kj
    valid = (dist >= 0) & (dist <= WINDOW)
    valid = jnp.stack([valid, valid & (kj >= BLOCK)])
    bucket = jnp.where(valid, _t5_bucket(dist)[None], -1).astype(jnp.int32)

    def out_map(v, h):
        return (v, h // GQA_GROUP, (h % GQA_GROUP) // 2, 0, h % 2)

    tab = pl.pallas_call(
        _bias_kernel,
        grid=(2, N_HEADS),
        in_specs=[pl.BlockSpec(memory_space=pltpu.SMEM),
                  pl.BlockSpec((None, BLOCK, 2 * BLOCK), lambda v, h: (v, 0, 0))],
        out_specs=pl.BlockSpec((None, None, None, BLOCK, 2 * BLOCK), out_map),
        out_shape=jax.ShapeDtypeStruct((2, N_KV_HEADS, PAIRS, BLOCK, 4 * BLOCK), F32),
        compiler_params=_params(2),
        name="bias",
    )(rel_bias.astype(F32).reshape(-1), bucket)
    return tab.reshape(2, N_KV_HEADS, PAIRS * BLOCK, 4 * BLOCK)


def _inproj_kernel(x_ref, sc_ref, sh_ref, g_ref, w_ref,
                   q_ref, kv_ref, u_ref, vg_ref, ga_ref, gb_ref, kvw_ref, h_scr,
                   *, seg, tiles_per_seq):
    i = pl.program_id(0)
    n = pl.program_id(1)

    @pl.when(n == 0)
    def _():
        y = _rms(x_ref[...]) * g_ref[...]
        h_scr[...] = (y * (1.0 + sc_ref[...]) + sh_ref[...]).astype(BF16)

    acc = jnp.dot(h_scr[...], w_ref[...], preferred_element_type=F32)
    n_q, n_kv, n_u, n_vg, n_ga = seg

    @pl.when(n < n_q)
    def _():
        q_ref[...] = (acc * Q_SCALE).astype(BF16)

    @pl.when(n == n_q)
    def _():
        kv_ref[...] = acc.astype(BF16)
        rows = kvw_ref.shape[0]

        @pl.when(i % tiles_per_seq == tiles_per_seq - 1)
        def _():
            kvw_ref[...] = acc[acc.shape[0] - rows:, :]

    @pl.when((n >= n_kv) & (n < n_u))
    def _():
        u_ref[...] = _gelu(acc).astype(BF16)

    @pl.when((n >= n_u) & (n < n_vg))
    def _():
        vg_ref[...] = _gelu(acc).astype(BF16)

    @pl.when((n >= n_vg) & (n < n_ga))
    def _():
        ga_ref[...] = jax.nn.sigmoid(acc).astype(BF16)

    @pl.when(n >= n_ga)
    def _():
        gb_ref[...] = jax.nn.sigmoid(acc).astype(BF16)


def _inproj(x, sc, sh, g, w_in, *, seq, per_row_mod):
    m, d = x.shape
    d_in = w_in.shape[1]
    d_sg = (d_in - D_ATTN - 2 * D_KV - 2 * d) // 2
    tn = 2 * D_KV
    assert D_ATTN % tn == 0 and d_sg % tn == 0 and d % tn == 0
    tm = _tile(seq, 1024)
    tiles_per_seq = seq // tm
    win = min(seq, WINDOW)
    n_q = D_ATTN // tn
    n_kv = n_q + 1
    n_u = n_kv + d_sg // tn
    n_vg = n_u + d_sg // tn
    n_ga = n_vg + d // tn
    n_all = d_in // tn

    def seg_map(lo, hi):
        return lambda i, n: (i, jnp.clip(n - lo, 0, hi - lo - 1))

    if per_row_mod:
        mod_spec = pl.BlockSpec((tm, d), lambda i, n: (i, 0))
    else:
        mod_spec = pl.BlockSpec((None, 1, d), lambda i, n: (i // tiles_per_seq, 0, 0))
    out_bf = lambda width: jax.ShapeDtypeStruct((m, width), BF16)
    return pl.pallas_call(
        functools.partial(_inproj_kernel, seg=(n_q, n_kv, n_u, n_vg, n_ga), tiles_per_seq=tiles_per_seq),
        grid=(m // tm, n_all),
        in_specs=[pl.BlockSpec((tm, d), lambda i, n: (i, 0)),
                  mod_spec, mod_spec,
                  pl.BlockSpec((1, d), lambda i, n: (0, 0)),
                  pl.BlockSpec((d, tn), lambda i, n: (0, n))],
        out_specs=[pl.BlockSpec((tm, tn), seg_map(0, n_q)),
                   pl.BlockSpec((tm, tn), lambda i, n: (i, 0)),
                   pl.BlockSpec((tm, tn), seg_map(n_kv, n_u)),
                   pl.BlockSpec((tm, tn), seg_map(n_u, n_vg)),
                   pl.BlockSpec((tm, tn), seg_map(n_vg, n_ga)),
                   pl.BlockSpec((tm, tn), seg_map(n_ga, n_all)),
                   pl.BlockSpec((None, win, tn), lambda i, n: (i // tiles_per_seq, 0, 0))],
        out_shape=[out_bf(D_ATTN), out_bf(tn), out_bf(d_sg), out_bf(d_sg), out_bf(d), out_bf(d),
                   jax.ShapeDtypeStruct((m // seq, win, tn), F32)],
        scratch_shapes=[pltpu.VMEM((tm, d), BF16)],
        compiler_params=_params(2),
        name="inproj",
    )(x, sc, sh, g.reshape(1, d), w_in)


def _layernorm_rows(v, g, b):
    vc = v - jnp.mean(v, axis=-1, keepdims=True)
    return vc * lax.rsqrt(jnp.mean(vc * vc, axis=-1, keepdims=True) + EPS) * g + b


def _mixer_kernel(sinks_ref, q_ref, kv_ref, kvp_ref, u_ref, vg_ref, ga_ref, gb_ref,
                  bias_ref, ws_ref, bs_ref, lng_ref, lnb_ref, o_ref,
                  kk_scr, wt_scr):
    j = pl.program_id(1)
    tq = q_ref.shape[0]
    kk_scr[0:BLOCK, :] = kvp_ref[...]
    kk_scr[BLOCK:, :] = kv_ref[...]
    row = lax.broadcasted_iota(jnp.int32, (SG_CHUNK, SG_CHUNK), 0)
    col = lax.broadcasted_iota(jnp.int32, (SG_CHUNK, SG_CHUNK), 1)
    for g in range(SG_GROUPS):
        wt_scr[g] = jnp.where(row >= col, ws_ref[g], 0.0).astype(BF16)
    zeros = jnp.zeros((2 * BLOCK, HEAD_DIM), BF16)
    ones = jnp.ones((2 * BLOCK, HEAD_DIM), BF16)
    low_lanes = lax.broadcasted_iota(jnp.int32, (BLOCK, 2 * HEAD_DIM), 1) < HEAD_DIM

    def block(bi, carry):
        r0 = pl.multiple_of(bi * BLOCK, BLOCK)
        rows = pl.ds(r0, BLOCK)
        kblk = kk_scr[pl.ds(r0, 2 * BLOCK), :]
        variant = ((j == 0) & (bi == 0)).astype(jnp.int32)
        vn = _layernorm_rows(vg_ref[rows, :].astype(F32), lng_ref[...], lnb_ref[...]).astype(BF16)

        def scores(g):
            kg = kblk[:, g * HEAD_DIM:(g + 1) * HEAD_DIM]
            kpad = jnp.concatenate([jnp.concatenate([kg, zeros], axis=1),
                                    jnp.concatenate([zeros, kg], axis=1)], axis=0)
            qg = jnp.concatenate([q_ref[rows, (g * PAIRS + p) * 2 * HEAD_DIM:(g * PAIRS + p + 1) * 2 * HEAD_DIM]
                                  for p in range(PAIRS)], axis=0)
            s = lax.dot_general(qg, kpad, (((1,), (1,)), ((), ())), preferred_element_type=F32)
            return s + bias_ref[variant, g]

        s_next = scores(0)
        for g in range(N_KV_HEADS):
            s = s_next
            if g + 1 < N_KV_HEADS:
                s_next = scores(g + 1)
            probs = ([], [])
            sink_terms = ([], [])
            for p in range(PAIRS):
                for e in range(2):
                    sink = sinks_ref[g * GQA_GROUP + 2 * p + e]
                    sh = s[p * BLOCK:(p + 1) * BLOCK, e * 2 * BLOCK:(e + 1) * 2 * BLOCK]
                    mx = jnp.maximum(jnp.max(sh, axis=-1, keepdims=True), sink)
                    probs[e].append(jnp.exp(sh - mx).astype(BF16))
                    sink_terms[e].append(jnp.exp(sink - mx))
            vv = kblk[:, D_KV + g * HEAD_DIM:D_KV + (g + 1) * HEAD_DIM]
            r_even = jnp.concatenate([vv, zeros, ones, zeros], axis=1)
            r_odd = jnp.concatenate([zeros, vv, zeros, ones], axis=1)
            out = (jnp.dot(jnp.concatenate(probs[0], axis=0), r_even, preferred_element_type=F32)
                   + jnp.dot(jnp.concatenate(probs[1], axis=0), r_odd, preferred_element_type=F32))
            for p in range(PAIRS):
                grp = g * PAIRS + p
                cs = slice(grp * SG_CHUNK, (grp + 1) * SG_CHUNK)
                o_pair = out[p * BLOCK:(p + 1) * BLOCK, :2 * HEAD_DIM]
                den = out[p * BLOCK:(p + 1) * BLOCK, 2 * HEAD_DIM:] + jnp.where(
                    low_lanes, sink_terms[0][p], sink_terms[1][p])
                ya = o_pair / den
                gate = jnp.dot(wt_scr[grp], vn[:, cs], preferred_element_type=F32) + bs_ref[grp]
                yb = u_ref[rows, cs].astype(F32) * gate
                merged = ga_ref[rows, cs].astype(F32) * ya + gb_ref[rows, cs].astype(F32) * yb
                o_ref[rows, cs] = merged.astype(BF16)
        return carry

    lax.fori_loop(0, tq // BLOCK, block, 0)


def _mixer_prompt(q, kv, u, vg, ga, gb, bias_tab, sinks, w_s, b_s, ln_g, ln_b, *, batch, seq):
    m, d = u.shape
    tq = _tile(seq, 512)
    steps = seq // tq
    blocks_per_tile = tq // BLOCK
    assert d == SG_GROUPS * SG_CHUNK and d == D_ATTN
    row_spec = lambda width: pl.BlockSpec((tq, width), lambda b, j: (b * steps + j, 0))
    const = lambda shape: pl.BlockSpec(shape, lambda b, j: (0,) * len(shape), pipeline_mode=pl.Buffered(1))
    bs_b = jnp.broadcast_to(b_s.astype(F32)[:, :, None], (SG_GROUPS, SG_CHUNK, SG_CHUNK))
    return pl.pallas_call(
        _mixer_kernel,
        grid=(batch, steps),
        in_specs=[pl.BlockSpec(memory_space=pltpu.SMEM),
                  row_spec(D_ATTN), row_spec(2 * D_KV),
                  pl.BlockSpec((BLOCK, 2 * D_KV),
                               lambda b, j: (jnp.maximum((b * steps + j) * blocks_per_tile - 1, 0), 0)),
                  row_spec(d), row_spec(d), row_spec(d), row_spec(d),
                  const(bias_tab.shape),
                  const((SG_GROUPS, SG_CHUNK, SG_CHUNK)),
                  const((SG_GROUPS, SG_CHUNK, SG_CHUNK)),
                  const((1, d)), const((1, d))],
        out_specs=row_spec(d),
        out_shape=jax.ShapeDtypeStruct((m, d), BF16),
        scratch_shapes=[pltpu.VMEM((tq + BLOCK, 2 * D_KV), BF16),
                        pltpu.VMEM((SG_GROUPS, SG_CHUNK, SG_CHUNK), BF16)],
        compiler_params=_params(2),
        name="mixer",
    )(sinks.astype(F32), q, kv, kv, u, vg, ga, gb, bias_tab, w_s.astype(F32), bs_b,
      ln_g.reshape(1, d), ln_b.reshape(1, d))


def _mixer_sample_kernel(q_ref, kn_ref, vn_ref, ck_ref, cv_ref, u_ref, vg_ref, ga_ref, gb_ref,
                         biasc_ref, biasn_ref, sinks_ref, wrow_ref, brow_ref, lng_ref, lnb_ref,
                         o_ref, sg_ref, ya_scr):
    for kvh in range(N_KV_HEADS):
        hs = slice(kvh * GQA_GROUP, (kvh + 1) * GQA_GROUP)
        ds_ = slice(kvh * HEAD_DIM, (kvh + 1) * HEAD_DIM)
        qg = q_ref[:, hs, :]
        kc = ck_ref[:, :, ds_].astype(BF16)
        s = jnp.einsum('bhd,bkd->bhk', qg.astype(BF16), kc, preferred_element_type=F32) + biasc_ref[hs, :][None]
        k_new = kn_ref[:, kvh:kvh + 1, :]
        s_new = jnp.sum(qg * k_new, axis=-1, keepdims=True) + biasn_ref[hs, :][None]
        sink = sinks_ref[hs, :][None]
        mx = jnp.maximum(jnp.maximum(jnp.max(s, axis=-1, keepdims=True), s_new), sink)
        p = jnp.exp(s - mx)
        p_new = jnp.exp(s_new - mx)
        den = jnp.sum(p, axis=-1, keepdims=True) + p_new + jnp.exp(sink - mx)
        vc = cv_ref[:, :, ds_].astype(BF16)
        o = jnp.einsum('bhk,bkd->bhd', p.astype(BF16), vc, preferred_element_type=F32)
        o = o + p_new * vn_ref[:, kvh:kvh + 1, :]
        ya_scr[:, hs, :] = o / den
    v = vg_ref[...]
    n_ch = v.shape[1] * v.shape[2]
    mean = jnp.sum(jnp.sum(v, axis=2, keepdims=True), axis=1, keepdims=True) / n_ch
    vc = v - mean
    var = jnp.sum(jnp.sum(vc * vc, axis=2, keepdims=True), axis=1, keepdims=True) / n_ch
    vn = vc * lax.rsqrt(var + EPS) * lng_ref[...][None] + lnb_ref[...][None]
    sg_ref[...] = vn
    yb = u_ref[...] * (wrow_ref[...][None] * vn + brow_ref[...][None])
    merged = ga_ref[...] * ya_scr[...] + gb_ref[...] * yb
    o_ref[...] = merged.astype(BF16)


def _mixer_sample(q, k_new, v_new, cache_k, cache_v, u, vg, ga, gb, bias_tab, sinks, w_s, b_s, ln_g, ln_b):
    nb, d = u.shape
    assert d == D_ATTN
    bt = _tile(nb, 16)
    heads = lambda a: a.astype(F32).reshape(nb, N_HEADS, HEAD_DIM)
    kvh = lambda a: a.reshape(nb, N_KV_HEADS, HEAD_DIM)
    per_head = lambda a: jnp.repeat(a.astype(F32), N_HEADS // SG_GROUPS).reshape(N_HEADS, 1)
    h_spec = pl.BlockSpec((bt, N_HEADS, HEAD_DIM), lambda b: (b, 0, 0))
    kv_spec = pl.BlockSpec((bt, N_KV_HEADS, HEAD_DIM), lambda b: (b, 0, 0))
    c_spec = pl.BlockSpec((bt, WINDOW, D_KV), lambda b: (b, 0, 0))
    const = lambda shape: pl.BlockSpec(shape, lambda b: (0, 0))
    bias_row0 = bias_tab[0].reshape(N_KV_HEADS, PAIRS, BLOCK, 2, 2 * BLOCK)[:, :, 0].reshape(N_HEADS, 2 * BLOCK)
    bias_c = bias_row0[:, :WINDOW]
    bias_n = bias_row0[:, WINDOW:WINDOW + 1]
    merged, sg_v = pl.pallas_call(
        _mixer_sample_kernel,
        grid=(nb // bt,),
        in_specs=[h_spec, kv_spec, kv_spec, c_spec, c_spec, h_spec, h_spec, h_spec, h_spec,
                  const((N_HEADS, WINDOW)), const((N_HEADS, 1)), const((N_HEADS, 1)),
                  const((N_HEADS, 1)), const((N_HEADS, 1)),
                  const((N_HEADS, HEAD_DIM)), const((N_HEADS, HEAD_DIM))],
        out_specs=[h_spec, h_spec],
        out_shape=[jax.ShapeDtypeStruct((nb, N_HEADS, HEAD_DIM), BF16),
                   jax.ShapeDtypeStruct((nb, N_HEADS, HEAD_DIM), F32)],
        scratch_shapes=[pltpu.VMEM((bt, N_HEADS, HEAD_DIM), F32)],
        compiler_params=_params(1),
        name="mixer_sample",
    )(heads(q), kvh(k_new), kvh(v_new), cache_k, cache_v, heads(u), heads(vg), heads(ga), heads(gb),
      bias_c, bias_n, sinks.astype(F32).reshape(N_HEADS, 1),
      per_head(w_s[:, 0, 0]), per_head(b_s[:, 0]),
      ln_g.astype(F32).reshape(N_HEADS, HEAD_DIM), ln_b.astype(F32).reshape(N_HEADS, HEAD_DIM))
    return merged.reshape(nb, d), sg_v.reshape(nb, d)


def _outproj_kernel(m_ref, w_ref, x_ref, gt_ref, gpost_ref, gpre_ref, sc_ref, sh_ref, x1_ref, h2_ref):
    o = jnp.dot(m_ref[...], w_ref[...], preferred_element_type=F32)
    x1 = x_ref[...] + gt_ref[...] * (_rms(o) * gpost_ref[...])
    x1_ref[...] = x1
    y = _rms(x1) * gpre_ref[...]
    h2_ref[...] = (y * (1.0 + sc_ref[...]) + sh_ref[...]).astype(BF16)


def _outproj(merged, w_o, x, gt, g_post, g_pre, sc, sh, *, seq, per_row_mod):
    m, d = x.shape
    tm = _tile(seq, 512)
    tiles_per_seq = seq // tm
    if per_row_mod:
        mod_spec = pl.BlockSpec((tm, d), lambda i: (i, 0))
    else:
        mod_spec = pl.BlockSpec((None, 1, d), lambda i: (i // tiles_per_seq, 0, 0))
    row_spec = pl.BlockSpec((tm, d), lambda i: (i, 0))
    vec_spec = pl.BlockSpec((1, d), lambda i: (0, 0))
    return pl.pallas_call(
        _outproj_kernel,
        grid=(m // tm,),
        in_specs=[row_spec, pl.BlockSpec((d, d), lambda i: (0, 0)), row_spec,
                  mod_spec, vec_spec, vec_spec, mod_spec, mod_spec],
        out_specs=[row_spec, row_spec],
        out_shape=[jax.ShapeDtypeStruct((m, d), F32), jax.ShapeDtypeStruct((m, d), BF16)],
        compiler_params=_params(1),
        name="outproj",
    )(merged, w_o, x, gt, g_post.reshape(1, d), g_pre.reshape(1, d), sc, sh)


def _ffn_start(f, acc_scr):
    @pl.when(f == 0)
    def _():
        acc_scr[...] = jnp.zeros(acc_scr.shape, F32)


def _ffn_finish(f, acc_scr, x1_ref, gt_ref, gpost_ref, y_ref):
    @pl.when(f == pl.num_programs(1) - 1)
    def _():
        y_ref[...] = x1_ref[...] + gt_ref[...] * (_rms(acc_scr[...]) * gpost_ref[...])


def _ffn_prompt_kernel(h_ref, wg_ref, wu_ref, cw_ref, cb_ref, wd_ref, x1_ref, gt_ref, gpost_ref,
                       y_ref, cs_ref, acc_scr, carry_scr, *, tiles_per_seq):
    i = pl.program_id(0)
    f = pl.program_id(1)
    _ffn_start(f, acc_scr)

    @pl.when(i % tiles_per_seq == 0)
    def _():
        carry_scr[f] = jnp.zeros(carry_scr.shape[1:], F32)

    h = h_ref[...]
    tm = h.shape[0]
    row = lax.broadcasted_iota(jnp.int32, (tm, MXU_COLS), 0)
    carry = carry_scr[f]
    acc = acc_scr[...]
    for c in range(wg_ref.shape[1] // MXU_COLS):
        cs = slice(c * MXU_COLS, (c + 1) * MXU_COLS)
        a = jnp.dot(h, wg_ref[:, cs], preferred_element_type=F32)
        u = jnp.dot(h, wu_ref[:, cs], preferred_element_type=F32)
        prev1 = carry[SUBLANES - 1:SUBLANES, cs]
        prev2 = carry[SUBLANES - 2:SUBLANES - 1, cs]
        a1 = jnp.where(row == 0, prev1, pltpu.roll(a, 1, axis=0))
        a2 = jnp.where(row == 0, prev2, jnp.where(row == 1, prev1, pltpu.roll(a, 2, axis=0)))
        conv = cb_ref[:, cs] + cw_ref[0:1, cs] * a2 + cw_ref[1:2, cs] * a1 + cw_ref[2:3, cs] * a
        carry_scr[f, :, cs] = a[tm - SUBLANES:, :]
        cs_ref[f, :, cs] = a[tm - (CONV_W - 1):, :]
        hid = (_gelu(conv) * u).astype(BF16)
        acc = acc + jnp.dot(hid, wd_ref[cs, :], preferred_element_type=F32)
    acc_scr[...] = acc
    _ffn_finish(f, acc_scr, x1_ref, gt_ref, gpost_ref, y_ref)


def _ffn_sample_kernel(h_ref, wg_ref, wu_ref, cw_ref, cb_ref, wd_ref, x1_ref, gt_ref, gpost_ref,
                       st0_ref, st1_ref, y_ref, a_ref, acc_scr):
    f = pl.program_id(1)
    _ffn_start(f, acc_scr)
    h = h_ref[...]
    a = jnp.dot(h, wg_ref[...], preferred_element_type=F32)
    u = jnp.dot(h, wu_ref[...], preferred_element_type=F32)
    conv = cb_ref[...] + cw_ref[0:1, :] * st0_ref[...] + cw_ref[1:2, :] * st1_ref[...] + cw_ref[2:3, :] * a
    a_ref[...] = a
    hid = (_gelu(conv) * u).astype(BF16)
    acc_scr[...] += jnp.dot(hid, wd_ref[...], preferred_element_type=F32)
    _ffn_finish(f, acc_scr, x1_ref, gt_ref, gpost_ref, y_ref)


def _ffn(h2, w_gate, w_up, conv_w, conv_b, w_down, x1, gt, g_post, *, seq, state=None):
    m, d = x1.shape
    d_ff = w_up.shape[1]
    tf = _tile(d_ff, 512)
    n_f = d_ff // tf
    tm = _tile(seq, 512)
    tiles_per_seq = seq // tm
    sample = state is not None
    if sample:
        mod_spec = pl.BlockSpec((tm, d), lambda i, f: (i, 0))
    else:
        mod_spec = pl.BlockSpec((None, 1, d), lambda i, f: (i // tiles_per_seq, 0, 0))
    row_spec = pl.BlockSpec((tm, d), lambda i, f: (i, 0))
    col_w = pl.BlockSpec((d, tf), lambda i, f: (0, f))
    in_specs = [row_spec, col_w, col_w,
                pl.BlockSpec((CONV_W, tf), lambda i, f: (0, f)),
                pl.BlockSpec((1, tf), lambda i, f: (0, f)),
                pl.BlockSpec((tf, d), lambda i, f: (f, 0)),
                row_spec, mod_spec, pl.BlockSpec((1, d), lambda i, f: (0, 0))]
    args = [h2, w_gate, w_up, conv_w.astype(F32), conv_b.astype(F32).reshape(1, d_ff), w_down,
            x1, gt, g_post.reshape(1, d)]
    scratch = [pltpu.VMEM((tm, d), F32)]
    if sample:
        st_spec = pl.BlockSpec((tm, tf), lambda i, f: (i, f))
        in_specs += [st_spec, st_spec]
        args += [state[:, 0, :], state[:, 1, :]]
        body = _ffn_sample_kernel
        out_specs = [row_spec, st_spec]
        out_shape = [jax.ShapeDtypeStruct((m, d), F32), jax.ShapeDtypeStruct((m, d_ff), F32)]
    else:
        body = functools.partial(_ffn_prompt_kernel, tiles_per_seq=tiles_per_seq)
        out_specs = [row_spec,
                     pl.BlockSpec((None, n_f, CONV_W - 1, tf), lambda i, f: (i // tiles_per_seq, 0, 0, 0))]
        out_shape = [jax.ShapeDtypeStruct((m, d), F32),
                     jax.ShapeDtypeStruct((m // seq, n_f, CONV_W - 1, tf), F32)]
        scratch.append(pltpu.VMEM((n_f, SUBLANES, tf), F32))
    y, aux = pl.pallas_call(
        body,
        grid=(m // tm, n_f),
        in_specs=in_specs,
        out_specs=out_specs,
        out_shape=out_shape,
        scratch_shapes=scratch,
        compiler_params=_params(2),
        name="ffn_sample" if sample else "ffn",
    )(*args)
    if not sample:
        aux = aux.transpose(0, 2, 1, 3).reshape(m // seq, CONV_W - 1, d_ff)
    return y, aux


def kernel(x_prompt, x_sample, c_prompt, c_sample, cache_k_win, cache_v_win, state_ffn_conv, w_ada, b_ada, g_pre_mix, g_post_mix, g_pre_ffn, g_post_ffn, w_in, w_o, sinks, rel_bias, sg_ln_g, sg_ln_b, w_s, b_s, w_up, w_gate, conv_w, conv_b, w_down):
    depth = w_in.shape[0]
    assert depth == 1, "single-layer step"
    batch, seq, d = x_prompt.shape
    nb, dec_seq, _ = x_sample.shape
    assert dec_seq == 1 and seq % BLOCK == 0
    l = 0

    n_rows = batch + nb
    pad = (-n_rows) % SUBLANES
    c_all = jnp.concatenate([c_prompt, c_sample, jnp.zeros((pad, d), F32)], axis=0)
    mod = _modulation(c_all, w_ada[l], b_ada[l])
    mods = jnp.split(mod, N_MOD, axis=-1)
    mp = [t[:batch].reshape(batch, 1, d) for t in mods]
    ms = [t[batch:n_rows] for t in mods]

    bias_tab = _bias_table(rel_bias)
    w_in_b = w_in[l].astype(BF16)
    w_o_b = w_o[l].astype(BF16)
    w_up_b = w_up[l].astype(BF16)
    w_gate_b = w_gate[l].astype(BF16)
    w_down_b = w_down[l].astype(BF16)

    xp = x_prompt.reshape(batch * seq, d)
    q, kv, u, vg, ga, gb, kv_win = _inproj(xp, mp[1], mp[0], g_pre_mix[l], w_in_b, seq=seq, per_row_mod=False)
    merged = _mixer_prompt(q, kv, u, vg, ga, gb, bias_tab, sinks[l], w_s[l], b_s[l], sg_ln_g[l], sg_ln_b[l],
                           batch=batch, seq=seq)
    x1, h2 = _outproj(merged, w_o_b, xp, mp[2], g_post_mix[l], g_pre_ffn[l], mp[4], mp[3],
                      seq=seq, per_row_mod=False)
    yp, conv_p = _ffn(h2, w_gate_b, w_up_b, conv_w[l], conv_b[l], w_down_b, x1, mp[5], g_post_ffn[l], seq=seq)
    y_prompt = yp.reshape(batch, seq, d)
    k_win_p = kv_win[:, :, :D_KV].reshape(1, batch, WINDOW, N_KV_HEADS, HEAD_DIM)
    v_win_p = kv_win[:, :, D_KV:].reshape(1, batch, WINDOW, N_KV_HEADS, HEAD_DIM)

    xs = x_sample.reshape(nb, d)
    q, kv, u, vg, ga, gb, kv_new = _inproj(xs, ms[1], ms[0], g_pre_mix[l], w_in_b, seq=nb, per_row_mod=True)
    kv_new = kv_new.reshape(nb, 2 * D_KV)
    k_new, v_new = kv_new[:, :D_KV], kv_new[:, D_KV:]
    ck = cache_k_win[l].reshape(nb, WINDOW, D_KV)
    cv = cache_v_win[l].reshape(nb, WINDOW, D_KV)
    merged, sg_v = _mixer_sample(q, k_new, v_new, ck, cv, u, vg, ga, gb, bias_tab, sinks[l], w_s[l], b_s[l],
                                 sg_ln_g[l], sg_ln_b[l])
    x1, h2 = _outproj(merged, w_o_b, xs, ms[2], g_post_mix[l], g_pre_ffn[l], ms[4], ms[3],
                      seq=nb, per_row_mod=True)
    ys, a_new = _ffn(h2, w_gate_b, w_up_b, conv_w[l], conv_b[l], w_down_b, x1, ms[5], g_post_ffn[l],
                     seq=nb, state=state_ffn_conv[l])
    y_sample = ys.reshape(nb, 1, d)
    k_win_s = jnp.concatenate([ck[:, 1:], k_new[:, None, :]], axis=1).reshape(1, nb, WINDOW, N_KV_HEADS, HEAD_DIM)
    v_win_s = jnp.concatenate([cv[:, 1:], v_new[:, None, :]], axis=1).reshape(1, nb, WINDOW, N_KV_HEADS, HEAD_DIM)
    conv_s = jnp.stack([state_ffn_conv[l][:, 1, :], a_new], axis=1)[None]
    sg_s = sg_v.reshape(1, nb, 1, d)

    return (y_prompt, y_sample, k_win_p, v_win_p, conv_p[None], k_win_s, v_win_s, conv_s, sg_s)
```
